```python
import jax
import jax.numpy as jnp
from jax import lax
import numpy as np

D_MODEL = 4096
BATCH = 1
SEQ = 16384
DEPTH = 1

N_META = 16
D_FF = ((8 * D_MODEL // 3 + 255) // 256) * 256
D_MIX = D_MODEL
HG_WIDTH = D_MIX // 2
HG_HEAD_DIM = 128
HG_HEADS = HG_WIDTH // HG_HEAD_DIM
MLA_WIDTH = D_MIX - HG_WIDTH
MLA_V_DIM = 128
MLA_HEADS = MLA_WIDTH // MLA_V_DIM
MLA_NOPE_DIM = 128
MLA_ROPE_DIM = 64
MLA_QK_DIM = MLA_NOPE_DIM + MLA_ROPE_DIM
Q_LORA_RANK = 1024
KV_LORA_RANK = 512
IN_SPLITS = (HG_WIDTH, HG_WIDTH, HG_WIDTH, HG_WIDTH, Q_LORA_RANK, KV_LORA_RANK, MLA_ROPE_DIM)
IN_COLS = sum(IN_SPLITS)
CHUNK = 64
Q_BLOCK = 128
ROPE_THETA = 10000.0
EPS = 1e-6

kernel_name = "hybrid_hgrn2_mla_macaron_meta"


def rms_norm(x, w):
    xf = x.astype(jnp.float32)
    y = xf * lax.rsqrt(jnp.mean(xf * xf, axis=-1, keepdims=True) + EPS)
    return (y * w.astype(jnp.float32)).astype(x.dtype)


def swiglu(x, w_gate, w_up, w_down):
    return (jax.nn.silu(x @ w_gate) * (x @ w_up)) @ w_down


def rope_tables(length):
    inv_freq = 1.0 / (ROPE_THETA ** (jnp.arange(0, MLA_ROPE_DIM, 2, dtype=jnp.float32) / MLA_ROPE_DIM))
    ang = jnp.arange(length, dtype=jnp.float32)[:, None] * inv_freq[None, :]
    return jnp.cos(ang), jnp.sin(ang)


def apply_rope(x, cos, sin):
    xf = x.astype(jnp.float32)
    x1, x2 = jnp.split(xf, 2, axis=-1)
    c = cos[None, :, None, :]
    s = sin[None, :, None, :]
    return jnp.concatenate([x1 * c - x2 * s, x2 * c + x1 * s], axis=-1).astype(x.dtype)


def gla_chunk(state, q, k, v, log_f):
    c = q.shape[1]
    b = jnp.cumsum(log_f, axis=1)
    o_inter = jnp.einsum('bthk,bhkv->bthv', q * jnp.exp(b), state)
    causal = jnp.tril(jnp.ones((c, c), dtype=bool))
    diff = b[:, :, None] - b[:, None, :]
    decay = jnp.exp(jnp.where(causal[None, :, :, None, None], diff, -jnp.inf))
    scores = jnp.einsum('bthk,bshk,btshk->bhts', q, k, decay)
    o_intra = jnp.einsum('bhts,bshv->bthv', scores, v)
    b_last = b[:, -1]
    k_dec = k * jnp.exp(b_last[:, None] - b)
    new_state = jnp.exp(b_last)[..., None] * state + jnp.einsum('bshk,bshv->bhkv', k_dec, v)
    return new_state, o_inter + o_intra


def hgrn2_recurrence(q, k, v, log_f):
    bsz, length, heads, dk = q.shape
    dv = v.shape[-1]
    state0 = jnp.zeros((bsz, heads, dk, dv), jnp.float32)
    state, o_meta = gla_chunk(state0, q[:, :N_META], k[:, :N_META], v[:, :N_META], log_f[:, :N_META])

    def to_chunks(t):
        return jnp.swapaxes(t[:, N_META:].reshape(bsz, -1, CHUNK, heads, t.shape[-1]), 0, 1)

    _, o_real = lax.scan(lambda s, inp: gla_chunk(s, *inp), state,
                         (to_chunks(q), to_chunks(k), to_chunks(v), to_chunks(log_f)))
    o_real = jnp.swapaxes(o_real, 0, 1).reshape(bsz, length - N_META, heads, dv)
    return jnp.concatenate([o_meta, o_real], axis=1)


def attend_block(q, k, v, q_pos, k_pos):
    s = jnp.einsum('bqhd,bkhd->bhqk', q, k).astype(jnp.float32) * (MLA_QK_DIM ** -0.5)
    s = jnp.where(k_pos[None, :] <= q_pos[:, None], s, -jnp.inf)
    p = jax.nn.softmax(s, axis=-1)
    return jnp.einsum('bhqk,bkhd->bqhd', p.astype(v.dtype), v)


def causal_attention(q, k, v):
    bsz, length, heads, dq = q.shape
    k_pos = jnp.arange(length)
    o_meta = attend_block(q[:, :N_META], k, v, jnp.arange(N_META), k_pos)
    n_blocks = (length - N_META) // Q_BLOCK
    q_blocks = jnp.swapaxes(q[:, N_META:].reshape(bsz, n_blocks, Q_BLOCK, heads, dq), 0, 1)
    q_pos = (N_META + jnp.arange(length - N_META)).reshape(n_blocks, Q_BLOCK)
    o_real = lax.map(lambda a: attend_block(a[0], k, v, a[1], k_pos), (q_blocks, q_pos))
    o_real = jnp.swapaxes(o_real, 0, 1).reshape(bsz, length - N_META, heads, v.shape[-1])
    return jnp.concatenate([o_meta, o_real], axis=1)


def hybrid_mixer(h, lb, w_in, hg_out_norm, mla_q_norm, mla_w_uq, mla_kv_norm, mla_w_ukv,
                 q_head_norm, k_head_norm, w_out, cos, sin):
    bsz, length, _ = h.shape
    u = h @ w_in
    split_points = [int(p) for p in np.cumsum(IN_SPLITS)[:-1]]
    hq, hf, hi, hg, cq, ckv, kr = jnp.split(u, split_points, axis=-1)

    def heads(t, n):
        return t.reshape(bsz, length, n, -1)

    lbh = lb.reshape(HG_HEADS, HG_HEAD_DIM)
    q_hg = jax.nn.silu(heads(hq, HG_HEADS).astype(jnp.float32))
    f = lbh + (1.0 - lbh) * jax.nn.sigmoid(heads(hf, HG_HEADS).astype(jnp.float32))
    k_hg = 1.0 - f
    v_hg = heads(hi, HG_HEADS).astype(jnp.float32)
    o_hg = hgrn2_recurrence(q_hg, k_hg, v_hg, jnp.log(f)).astype(h.dtype)
    o_hg = rms_norm(o_hg, hg_out_norm) * jax.nn.silu(heads(hg, HG_HEADS))

    q_m = heads(rms_norm(cq, mla_q_norm) @ mla_w_uq, MLA_HEADS)
    kv = heads(rms_norm(ckv, mla_kv_norm) @ mla_w_ukv, MLA_HEADS)
    k_nope, v_m = jnp.split(kv, [MLA_NOPE_DIM], axis=-1)
    k_rope = jnp.broadcast_to(kr[:, :, None, :], (bsz, length, MLA_HEADS, MLA_ROPE_DIM))
    k_m = jnp.concatenate([k_nope, k_rope], axis=-1)
    q_m = rms_norm(q_m, q_head_norm)
    k_m = rms_norm(k_m, k_head_norm)
    q_m = jnp.concatenate([q_m[..., :MLA_NOPE_DIM], apply_rope(q_m[..., MLA_NOPE_DIM:], cos, sin)], axis=-1)
    k_m = jnp.concatenate([k_m[..., :MLA_NOPE_DIM], apply_rope(k_m[..., MLA_NOPE_DIM:], cos, sin)], axis=-1)
    o_m = causal_attention(q_m, k_m, v_m)

    mixed = jnp.concatenate([o_hg.reshape(bsz, length, HG_WIDTH),
                             o_m.reshape(bsz, length, MLA_WIDTH)], axis=-1)
    return mixed @ w_out


def _normal(k, shape, scale):
    return jax.random.normal(k, shape, jnp.float32) * scale


def setup_inputs(seed: int = 0) -> dict:
    key = jax.random.key(seed)
    ks = jax.random.split(key, 24)
    gain = lambda k, shape: 1.0 + _normal(k, shape, 0.02)
    return {
        "x": _normal(ks[0], (BATCH, SEQ, D_MODEL), 1.0),
        "meta_tokens": _normal(ks[1], (N_META, D_MODEL), 1.0),
        "lb_param": _normal(ks[2], (DEPTH + 1, HG_WIDTH), 0.5),
        "ffn1_norm": gain(ks[3], (DEPTH, D_MODEL)),
        "ffn1_w_gate": _normal(ks[4], (DEPTH, D_MODEL, D_FF), D_MODEL ** -0.5),
        "ffn1_w_up": _normal(ks[5], (DEPTH, D_MODEL, D_FF), D_MODEL ** -0.5),
        "ffn1_w_down": _normal(ks[6], (DEPTH, D_FF, D_MODEL), D_FF ** -0.5),
        "mix_norm": gain(ks[7], (DEPTH, D_MODEL)),
        "w_in": _normal(ks[8], (DEPTH, D_MODEL, IN_COLS), D_MODEL ** -0.5),
        "hg_out_norm": gain(ks[9], (DEPTH, HG_HEADS, HG_HEAD_DIM)),
        "mla_q_norm": gain(ks[10], (DEPTH, Q_LORA_RANK)),
        "mla_w_uq": _normal(ks[11], (DEPTH, Q_LORA_RANK, MLA_HEADS * MLA_QK_DIM), Q_LORA_RANK ** -0.5),
        "mla_kv_norm": gain(ks[12], (DEPTH, KV_LORA_RANK)),
        "mla_w_ukv": _normal(ks[13], (DEPTH, KV_LORA_RANK, MLA_HEADS * (MLA_NOPE_DIM + MLA_V_DIM)), KV_LORA_RANK ** -0.5),
        "q_head_norm": gain(ks[14], (DEPTH, MLA_QK_DIM)),
        "k_head_norm": gain(ks[15], (DEPTH, MLA_QK_DIM)),
        "w_out": _normal(ks[16], (DEPTH, D_MIX, D_MODEL), D_MIX ** -0.5),
        "ffn2_norm": gain(ks[17], (DEPTH, D_MODEL)),
        "ffn2_w_gate": _normal(ks[18], (DEPTH, D_MODEL, D_FF), D_MODEL ** -0.5),
        "ffn2_w_up": _normal(ks[19], (DEPTH, D_MODEL, D_FF), D_MODEL ** -0.5),
        "ffn2_w_down": _normal(ks[20], (DEPTH, D_FF, D_MODEL), D_FF ** -0.5),
        "out_norm": gain(ks[21], (DEPTH, D_MODEL)),
    }


def reference(x, meta_tokens, lb_param, ffn1_norm, ffn1_w_gate, ffn1_w_up, ffn1_w_down,
              mix_norm, w_in, hg_out_norm, mla_q_norm, mla_w_uq, mla_kv_norm, mla_w_ukv,
              q_head_norm, k_head_norm, w_out, ffn2_norm, ffn2_w_gate, ffn2_w_up, ffn2_w_down,
              out_norm):
    bsz = x.shape[0]
    meta = jnp.broadcast_to(meta_tokens.astype(x.dtype)[None], (bsz, N_META, D_MODEL))
    h = jnp.concatenate([meta, x], axis=1)
    cos, sin = rope_tables(h.shape[1])
    lb_all = jnp.cumsum(jax.nn.softmax(lb_param.astype(jnp.float32), axis=0), axis=0)
    for l in range(DEPTH):
        h = h + 0.5 * swiglu(rms_norm(h, ffn1_norm[l]), ffn1_w_gate[l], ffn1_w_up[l], ffn1_w_down[l])
        h = h + hybrid_mixer(rms_norm(h, mix_norm[l]), lb_all[l], w_in[l], hg_out_norm[l],
                             mla_q_norm[l], mla_w_uq[l], mla_kv_norm[l], mla_w_ukv[l],
                             q_head_norm[l], k_head_norm[l], w_out[l], cos, sin)
        h = h + 0.5 * swiglu(rms_norm(h, ffn2_norm[l]), ffn2_w_gate[l], ffn2_w_up[l], ffn2_w_down[l])
        h = rms_norm(h, out_norm[l])
    return h[:, N_META:]
```

```python
import functools
import math

import jax
import jax.numpy as jnp
from jax import lax
from jax.experimental import pallas as pl
from jax.experimental.pallas import tpu as pltpu

F32 = jnp.float32
BF16 = jnp.bfloat16

EPS = 1e-6
ROPE_THETA = 10000.0

V7X_LANES = 128
V7X_SUBLANES_BF16 = 16
V7X_SCOPED_VMEM_BYTES = 60000 * 1024
MASK_VALUE = -1e30

HGRN_SUB = 16


def _tile(dim, target, align):
    t = min(target, dim) // align * align
    while t >= align:
        if dim % t == 0:
            return t
        t -= align
    return dim


def _vmem_limit(*buffer_bytes):
    need = int(sum(buffer_bytes) * 1.25) + (4 << 20)
    return min(max(need, 16 << 20), V7X_SCOPED_VMEM_BYTES)


def _nbytes(shape, dtype):
    return math.prod(shape) * jnp.dtype(dtype).itemsize


def _dot(a, b):
    return jnp.dot(a, b, preferred_element_type=F32)


def _dot_nt(a, b):
    return lax.dot_general(a, b, (((1,), (1,)), ((), ())), preferred_element_type=F32)


def _dot_tn(a, b):
    return lax.dot_general(a, b, (((0,), (0,)), ((), ())), preferred_element_type=F32)


def _rmsnorm_kernel(x_ref, w_ref, o_ref):
    x = x_ref[...].astype(F32)
    ms = jnp.mean(x * x, axis=-1, keepdims=True)
    o_ref[...] = (x * lax.rsqrt(ms + EPS) * w_ref[...]).astype(o_ref.dtype)


def _rmsnorm(x, w, *, out_dtype, col_block=0, ncols=None, rows=None):
    m = x.shape[0] if rows is None else rows
    c = x.shape[1] if ncols is None else ncols
    tm = _tile(m, 512, V7X_SUBLANES_BF16)
    blk = (tm, c)
    return pl.pallas_call(
        _rmsnorm_kernel,
        out_shape=jax.ShapeDtypeStruct((m, c), out_dtype),
        grid=(m // tm,),
        in_specs=[pl.BlockSpec(blk, lambda i: (i, col_block)),
                  pl.BlockSpec((1, c), lambda i: (0, 0))],
        out_specs=pl.BlockSpec(blk, lambda i: (i, 0)),
        compiler_params=pltpu.CompilerParams(
            dimension_semantics=("parallel",),
            vmem_limit_bytes=_vmem_limit(2 * _nbytes(blk, x.dtype), 2 * _nbytes(blk, out_dtype),
                                         2 * _nbytes(blk, F32))),
        name="rmsnorm",
    )(x, w.reshape(1, c).astype(F32))


def _mm_kernel(a_ref, b_ref, o_ref):
    o_ref[...] = _dot(a_ref[...], b_ref[...]).astype(o_ref.dtype)


def _mm_res_kernel(a_ref, b_ref, r_ref, o_ref, *, scale):
    o_ref[...] = r_ref[...] + scale * _dot(a_ref[...], b_ref[...])


def _mm2_res_kernel(a1_ref, a2_ref, b1_ref, b2_ref, r_ref, o_ref):
    o_ref[...] = r_ref[...] + (_dot(a1_ref[...], b1_ref[...]) + _dot(a2_ref[...], b2_ref[...]))


def _gateup_kernel(x_ref, wg_ref, wu_ref, o_ref):
    x = x_ref[...]
    g = _dot(x, wg_ref[...])
    u = _dot(x, wu_ref[...])
    o_ref[...] = (g * jax.nn.sigmoid(g) * u).astype(o_ref.dtype)


def _a_spec(tm, k, single_buffer):
    if single_buffer:
        return pl.BlockSpec((tm, k), lambda i, j: (i, 0), pipeline_mode=pl.Buffered(1))
    return pl.BlockSpec((tm, k), lambda i, j: (i, 0))


def _matmul(a, b, *, out_dtype, rows=None, res=None, scale=1.0, tm_target=1024, tn_target=1024,
            name="matmul"):
    m = a.shape[0] if rows is None else rows
    k, n = b.shape
    tm = _tile(m, tm_target, V7X_SUBLANES_BF16)
    tn = _tile(n, tn_target, V7X_LANES)
    a_bytes = _nbytes((tm, k), a.dtype)
    single = a_bytes > (12 << 20)
    in_specs = [_a_spec(tm, k, single), pl.BlockSpec((k, tn), lambda i, j: (0, j))]
    args = [a, b]
    vm = [a_bytes * (1 if single else 2), 2 * _nbytes((k, tn), b.dtype),
          2 * _nbytes((tm, tn), out_dtype), _nbytes((tm, tn), F32)]
    if res is None:
        kern = _mm_kernel
    else:
        kern = functools.partial(_mm_res_kernel, scale=scale)
        in_specs.append(pl.BlockSpec((tm, tn), lambda i, j: (i, j)))
        args.append(res)
        vm.append(2 * _nbytes((tm, tn), res.dtype))
    return pl.pallas_call(
        kern,
        out_shape=jax.ShapeDtypeStruct((m, n), out_dtype),
        grid=(m // tm, n // tn),
        in_specs=in_specs,
        out_specs=pl.BlockSpec((tm, tn), lambda i, j: (i, j)),
        compiler_params=pltpu.CompilerParams(
            dimension_semantics=("parallel", "parallel"), vmem_limit_bytes=_vmem_limit(*vm)),
        name=name,
    )(*args)


def _matmul2_res(a1, a2, b1, b2, res, *, rows):
    m = rows
    k1, n = b1.shape
    k2 = b2.shape[0]
    tm = _tile(m, 1024, V7X_SUBLANES_BF16)
    tn = _tile(n, 1024, V7X_LANES)
    vm = [2 * _nbytes((tm, k1 + k2), a1.dtype), 2 * _nbytes((k1 + k2, tn), b1.dtype),
          5 * _nbytes((tm, tn), F32)]
    return pl.pallas_call(
        _mm2_res_kernel,
        out_shape=jax.ShapeDtypeStruct((m, n), F32),
        grid=(m // tm, n // tn),
        in_specs=[pl.BlockSpec((tm, k1), lambda i, j: (i, 0)),
                  pl.BlockSpec((tm, k2), lambda i, j: (i, 0)),
                  pl.BlockSpec((k1, tn), lambda i, j: (0, j)),
                  pl.BlockSpec((k2, tn), lambda i, j: (0, j)),
                  pl.BlockSpec((tm, tn), lambda i, j: (i, j))],
        out_specs=pl.BlockSpec((tm, tn), lambda i, j: (i, j)),
        compiler_params=pltpu.CompilerParams(
            dimension_semantics=("parallel", "parallel"), vmem_limit_bytes=_vmem_limit(*vm)),
        name="out_proj",
    )(a1, a2, b1, b2, res)


def _gateup(x, wg, wu):
    m, k = x.shape
    f = wg.shape[1]
    tm = _tile(m, 1024, V7X_SUBLANES_BF16)
    tn = _tile(f, 256, V7X_LANES)
    vm = [2 * _nbytes((tm, k), x.dtype), 4 * _nbytes((k, tn), wg.dtype),
          2 * _nbytes((tm, tn), BF16), 3 * _nbytes((tm, tn), F32)]
    return pl.pallas_call(
        _gateup_kernel,
        out_shape=jax.ShapeDtypeStruct((m, f), BF16),
        grid=(m // tm, f // tn),
        in_specs=[pl.BlockSpec((tm, k), lambda i, j: (i, 0)),
                  pl.BlockSpec((k, tn), lambda i, j: (0, j)),
                  pl.BlockSpec((k, tn), lambda i, j: (0, j))],
        out_specs=pl.BlockSpec((tm, tn), lambda i, j: (i, j)),
        compiler_params=pltpu.CompilerParams(
            dimension_semantics=("parallel", "parallel"), vmem_limit_bytes=_vmem_limit(*vm)),
        name="ffn_gateup",
    )(x, wg, wu)


def _swiglu_half_step(h, norm_w, wg, wu, wd):
    xn = _rmsnorm(h, norm_w, out_dtype=BF16)
    act = _gateup(xn, wg, wu)
    return _matmul(act, wd, out_dtype=F32, res=h, scale=0.5, tn_target=256, name="ffn_down")


def _cumsum8(x, row):
    for sh in (1, 2, 4):
        x = x + jnp.where(row >= sh, pltpu.roll(x, sh, 0), 0.0)
    return x


def _gates(hf, lb):
    f = lb + (1.0 - lb) * jax.nn.sigmoid(hf)
    return 1.0 - f, jnp.log(f)


def _log_decay(logf, row8):
    b_top = _cumsum8(logf[0:8], row8)
    b_bot = _cumsum8(logf[8:16], row8) + b_top[7:8]
    return b_top, b_bot


def _state_update(st_ref, kd, v, dec, heads, hd):
    for h in range(heads):
        sl = slice(h * hd, (h + 1) * hd)
        upd = _dot_tn(v[:, sl].astype(BF16), kd[:, sl].astype(BF16))
        if dec is None:
            st_ref[h] = upd
        else:
            st_ref[h] = st_ref[h] * dec[:, sl] + upd


def _hgrn_kernel(uq_ref, uf_ref, ui_ref, ug_ref, mf_ref, mi_ref, lb_ref, gain_ref, o_ref, st_ref,
                 *, heads, hd, nsub):
    lb = lb_ref[...]
    gain = gain_ref[...]
    row8 = lax.broadcasted_iota(jnp.int32, (8, heads * hd), 0)
    row8h = row8[:, :hd]

    @pl.when(pl.program_id(0) == 0)
    def _():
        k, logf = _gates(mf_ref[...], lb)
        b_top, b_bot = _log_decay(logf, row8)
        b = jnp.concatenate([b_top, b_bot], axis=0)
        kd = k * jnp.exp(b_bot[7:8] - b)
        _state_update(st_ref, kd, mi_ref[...], None, heads, hd)

    def sub_chunk(c, carry):
        r0 = pl.multiple_of(c * HGRN_SUB, HGRN_SUB)
        rows = pl.ds(r0, HGRN_SUB)
        hq = uq_ref[rows, :]
        q = hq * jax.nn.sigmoid(hq)
        k, logf = _gates(uf_ref[rows, :], lb)
        v = ui_ref[rows, :]
        hg = ug_ref[rows, :]
        gate = hg * jax.nn.sigmoid(hg)
        b_top, b_bot = _log_decay(logf, row8)
        b = jnp.concatenate([b_top, b_bot], axis=0)
        b_last = b_bot[7:8]
        qe = (q * jnp.exp(b)).astype(BF16)
        for h in range(heads):
            sl = slice(h * hd, (h + 1) * hd)
            o = _dot_nt(qe[:, sl], st_ref[h].astype(BF16))
            o_top, o_bot = o[0:8], o[8:16]
            qh, kh, vh = q[:, sl], k[:, sl], v[:, sl]
            bt, bb = b_top[:, sl], b_bot[:, sl]
            for s in range(HGRN_SUB):
                bs = (bt if s < 8 else bb)[s % 8:s % 8 + 1]
                ks, vs = kh[s:s + 1], vh[s:s + 1]
                if s < 8:
                    e = jnp.exp(jnp.where(row8h >= s, bt - bs, MASK_VALUE))
                    o_top = o_top + jnp.sum(qh[0:8] * ks * e, axis=-1, keepdims=True) * vs
                    e = jnp.exp(bb - bs)
                else:
                    e = jnp.exp(jnp.where(row8h >= s - 8, bb - bs, MASK_VALUE))
                o_bot = o_bot + jnp.sum(qh[8:16] * ks * e, axis=-1, keepdims=True) * vs
            o = jnp.concatenate([o_top, o_bot], axis=0)
            ms = jnp.mean(o * o, axis=-1, keepdims=True)
            y = o * lax.rsqrt(ms + EPS) * gain[:, sl] * gate[:, sl]
            o_ref[rows, sl] = y.astype(o_ref.dtype)
        kd = k * jnp.exp(b_last - b)
        _state_update(st_ref, kd, v, jnp.exp(b_last), heads, hd)
        return carry

    lax.fori_loop(0, nsub, sub_chunk, 0)


def _hgrn2(u, u_meta, lb, gain, *, rows, width, heads):
    hd = width // heads
    assert u_meta.shape[0] == HGRN_SUB and hd % V7X_LANES == 0 and rows % HGRN_SUB == 0
    tb = _tile(rows, 256, HGRN_SUB)
    blk = (tb, width)
    mblk = (HGRN_SUB, width)
    vm = [8 * _nbytes(blk, F32), 2 * _nbytes(blk, BF16), heads * hd * hd * 4, 8 * _nbytes(mblk, F32)]
    return pl.pallas_call(
        functools.partial(_hgrn_kernel, heads=heads, hd=hd, nsub=tb // HGRN_SUB),
        out_shape=jax.ShapeDtypeStruct((rows, width), BF16),
        grid=(rows // tb,),
        in_specs=[pl.BlockSpec(blk, lambda i: (i, 0)),
                  pl.BlockSpec(blk, lambda i: (i, 1)),
                  pl.BlockSpec(blk, lambda i: (i, 2)),
                  pl.BlockSpec(blk, lambda i: (i, 3)),
                  pl.BlockSpec(mblk, lambda i: (0, 1)),
                  pl.BlockSpec(mblk, lambda i: (0, 2)),
                  pl.BlockSpec((1, width), lambda i: (0, 0)),
                  pl.BlockSpec((1, width), lambda i: (0, 0))],
        out_specs=pl.BlockSpec(blk, lambda i: (i, 0)),
        scratch_shapes=[pltpu.VMEM((heads, hd, hd), F32)],
        compiler_params=pltpu.CompilerParams(
            dimension_semantics=("arbitrary",), vmem_limit_bytes=_vmem_limit(*vm)),
        name="hgrn2",
    )(u, u, u, u, u_meta, u_meta, lb.reshape(1, width), gain.reshape(1, width))


def _swap_halves(x, lane, half):
    return jnp.where(lane < half, pltpu.roll(x, V7X_LANES - half, 1), pltpu.roll(x, half, 1))


def _qprep_kernel(x_ref, c_ref, s_ref, w_ref, o_ref, *, heads, nope, dqk, half, scale):
    c, sn, w = c_ref[...], s_ref[...], w_ref[...]
    lane = lax.broadcasted_iota(jnp.int32, (1, V7X_LANES), 1)
    hp = nope + V7X_LANES
    for h in range(heads):
        xn = x_ref[:, h * hp:h * hp + nope]
        xr = x_ref[:, h * hp + nope:(h + 1) * hp]
        ss = jnp.sum(xn * xn, axis=-1, keepdims=True) + jnp.sum(xr * xr, axis=-1, keepdims=True)
        r = lax.rsqrt(ss * (1.0 / dqk) + EPS) * scale
        xr = xr * r * w[:, nope:]
        o_ref[h, :, 0:nope] = (xn * r * w[:, :nope]).astype(o_ref.dtype)
        o_ref[h, :, nope:hp] = (xr * c + _swap_halves(xr, lane, half) * sn).astype(o_ref.dtype)


def _kprep_kernel(kv_ref, kr_ref, c_ref, s_ref, w_ref, k_ref, v_ref, *, heads, nope, vdim, dqk, half):
    c, sn, w = c_ref[...], s_ref[...], w_ref[...]
    lane = lax.broadcasted_iota(jnp.int32, (1, V7X_LANES), 1)
    kr = kr_ref[...]
    ss_r = jnp.sum(kr * kr, axis=-1, keepdims=True)
    krw = kr * w[:, nope:]
    kr_rot = krw * c + _swap_halves(krw, lane, half) * sn
    for h in range(heads):
        kn = kv_ref[:, h * nope:(h + 1) * nope]
        r = lax.rsqrt((jnp.sum(kn * kn, axis=-1, keepdims=True) + ss_r) * (1.0 / dqk) + EPS)
        k_ref[h, :, 0:nope] = (kn * r * w[:, :nope]).astype(k_ref.dtype)
        k_ref[h, :, nope:nope + V7X_LANES] = (kr_rot * r).astype(k_ref.dtype)
        v_ref[h] = kv_ref[:, heads * nope + h * vdim:heads * nope + (h + 1) * vdim].astype(v_ref.dtype)


def _rope_tables(pos0, n, rope_dim):
    half = rope_dim // 2
    inv_freq = 1.0 / (ROPE_THETA ** (jnp.arange(0, rope_dim, 2, dtype=F32) / rope_dim))
    ang = (pos0 + jnp.arange(n, dtype=F32))[:, None] * inv_freq[None, :]
    c, s = jnp.cos(ang), jnp.sin(ang)
    z = jnp.zeros((n, V7X_LANES - rope_dim), F32)
    return jnp.concatenate([c, c, z], axis=1), jnp.concatenate([-s, s, z], axis=1)


def _qprep(qraw, cos_t, sin_t, w, *, heads, nope, dqk, rope_dim):
    m = qraw.shape[0]
    hp = nope + V7X_LANES
    tm = _tile(m, 512, V7X_SUBLANES_BF16)
    vm = [2 * _nbytes((tm, heads * hp), F32), 2 * _nbytes((heads, tm, hp), BF16), 4 * _nbytes((tm, hp), F32)]
    return pl.pallas_call(
        functools.partial(_qprep_kernel, heads=heads, nope=nope, dqk=dqk, half=rope_dim // 2,
                          scale=dqk ** -0.5),
        out_shape=jax.ShapeDtypeStruct((heads, m, hp), BF16),
        grid=(m // tm,),
        in_specs=[pl.BlockSpec((tm, heads * hp), lambda i: (i, 0)),
                  pl.BlockSpec((tm, V7X_LANES), lambda i: (i, 0)),
                  pl.BlockSpec((tm, V7X_LANES), lambda i: (i, 0)),
                  pl.BlockSpec((1, hp), lambda i: (0, 0))],
        out_specs=pl.BlockSpec((heads, tm, hp), lambda i: (0, i, 0)),
        compiler_params=pltpu.CompilerParams(
            dimension_semantics=("parallel",), vmem_limit_bytes=_vmem_limit(*vm)),
        name="mla_q_prep",
    )(qraw, cos_t, sin_t, w)


def _kprep(kv, u, kr_block, cos_t, sin_t, w, *, heads, nope, vdim, dqk, rope_dim):
    m = kv.shape[0]
    hp = nope + V7X_LANES
    tm = _tile(m, 512, V7X_SUBLANES_BF16)
    vm = [2 * _nbytes((tm, kv.shape[1]), F32), 2 * _nbytes((heads, tm, hp + vdim), BF16),
          4 * _nbytes((tm, hp), F32)]
    return pl.pallas_call(
        functools.partial(_kprep_kernel, heads=heads, nope=nope, vdim=vdim, dqk=dqk, half=rope_dim // 2),
        out_shape=(jax.ShapeDtypeStruct((heads, m, hp), BF16),
                   jax.ShapeDtypeStruct((heads, m, vdim), BF16)),
        grid=(m // tm,),
        in_specs=[pl.BlockSpec((tm, kv.shape[1]), lambda i: (i, 0)),
                  pl.BlockSpec((tm, V7X_LANES), lambda i: (i, kr_block)),
                  pl.BlockSpec((tm, V7X_LANES), lambda i: (i, 0)),
                  pl.BlockSpec((tm, V7X_LANES), lambda i: (i, 0)),
                  pl.BlockSpec((1, hp), lambda i: (0, 0))],
        out_specs=(pl.BlockSpec((heads, tm, hp), lambda i: (0, i, 0)),
                   pl.BlockSpec((heads, tm, vdim), lambda i: (0, i, 0))),
        compiler_params=pltpu.CompilerParams(
            dimension_semantics=("parallel",), vmem_limit_bytes=_vmem_limit(*vm)),
        name="mla_kv_prep",
    )(kv, u, cos_t, sin_t, w)


def _attn_kernel(q_ref, k_ref, v_ref, km_ref, vm_ref, o_ref, *, tq, tk, n_meta):
    qi = pl.program_id(1)
    q = q_ref[0]

    def online(carry, s, v):
        m, l, acc = carry
        m_new = jnp.maximum(m, jnp.max(s, axis=-1, keepdims=True))
        alpha = jnp.exp(m - m_new)
        p = jnp.exp(s - m_new)
        l = alpha * l + jnp.sum(p, axis=-1, keepdims=True)
        acc = alpha * acc + _dot(p.astype(v.dtype), v)
        return m_new, l, acc

    s = _dot_nt(q, km_ref[0])
    col = lax.broadcasted_iota(jnp.int32, s.shape, 1)
    s = jnp.where(col < n_meta, s, MASK_VALUE)
    m = jnp.max(s, axis=-1, keepdims=True)
    p = jnp.exp(s - m)
    carry = (m, jnp.sum(p, axis=-1, keepdims=True), _dot(p.astype(BF16), vm_ref[0]))

    def block(j, carry):
        rows = pl.ds(pl.multiple_of(j * tk, tk), tk)
        return online(carry, _dot_nt(q, k_ref[0, rows, :]), v_ref[0, rows, :])

    per = tq // tk
    carry = lax.fori_loop(0, qi * per, block, carry)

    row = lax.broadcasted_iota(jnp.int32, (tq, tk), 0)
    col = lax.broadcasted_iota(jnp.int32, (tq, tk), 1)
    for d in range(per):
        rows = pl.ds(pl.multiple_of((qi * per + d) * tk, tk), tk)
        s = _dot_nt(q, k_ref[0, rows, :])
        s = jnp.where(col + d * tk <= row, s, MASK_VALUE)
        carry = online(carry, s, v_ref[0, rows, :])

    m, l, acc = carry
    o_ref[...] = (acc / l).astype(o_ref.dtype)


def _attention(q, k, v, k_meta, v_meta, *, n_meta):
    heads, t, dk = q.shape
    vdim = v.shape[-1]
    mp = k_meta.shape[1]
    tq = _tile(t, 1024, V7X_SUBLANES_BF16)
    tk = _tile(tq, 512, V7X_SUBLANES_BF16)
    vm = [2 * _nbytes((t, dk), BF16), 2 * _nbytes((t, vdim), BF16), 4 * _nbytes((tq, dk), BF16),
          4 * _nbytes((tq, tk), F32), 4 * _nbytes((tq, vdim), F32)]
    return pl.pallas_call(
        functools.partial(_attn_kernel, tq=tq, tk=tk, n_meta=n_meta),
        out_shape=jax.ShapeDtypeStruct((t, heads * vdim), BF16),
        grid=(heads, t // tq),
        in_specs=[pl.BlockSpec((1, tq, dk), lambda h, i: (h, i, 0)),
                  pl.BlockSpec((1, t, dk), lambda h, i: (h, 0, 0)),
                  pl.BlockSpec((1, t, vdim), lambda h, i: (h, 0, 0)),
                  pl.BlockSpec((1, mp, dk), lambda h, i: (h, 0, 0)),
                  pl.BlockSpec((1, mp, vdim), lambda h, i: (h, 0, 0))],
        out_specs=pl.BlockSpec((tq, vdim), lambda h, i: (i, h)),
        compiler_params=pltpu.CompilerParams(
            dimension_semantics=("parallel", "parallel"), vmem_limit_bytes=_vmem_limit(*vm)),
        name="mla_attention",
    )(q, k, v, k_meta, v_meta)


def _pad_cols(w, n):
    return jnp.pad(w, ((0, 0), (0, n - w.shape[1])))


def _mixer(h, h_meta, lb, mix_norm, w_in, hg_out_norm, mla_q_norm, w_uq, mla_kv_norm, w_ukv,
           q_head_norm, k_head_norm, w_out):
    t, d = h.shape
    n_meta = h_meta.shape[0]
    heads_hg, hd = hg_out_norm.shape
    width = heads_hg * hd
    q_rank, kv_rank = mla_q_norm.shape[0], mla_kv_norm.shape[0]
    dqk = q_head_norm.shape[0]
    heads = w_uq.shape[1] // dqk
    rope_dim = w_in.shape[1] - 4 * width - q_rank - kv_rank
    nope = dqk - rope_dim
    vdim = w_ukv.shape[1] // heads - nope
    hp = nope + V7X_LANES
    assert nope % V7X_LANES == 0 and vdim % V7X_LANES == 0 and rope_dim <= V7X_LANES
    assert q_rank % kv_rank == 0 and (4 * width + q_rank) % kv_rank == 0 and width % V7X_LANES == 0

    mla_cols = q_rank + kv_rank + V7X_LANES
    w_in_hg = w_in[:, :4 * width].astype(BF16)
    w_in_mla = _pad_cols(w_in[:, 4 * width:], mla_cols).astype(BF16)
    w_uq_p = jnp.pad(w_uq.reshape(q_rank, heads, dqk), ((0, 0), (0, 0), (0, hp - dqk)))
    w_uq_p = w_uq_p.reshape(q_rank, heads * hp).astype(BF16)
    w_ukv_h = w_ukv.reshape(kv_rank, heads, nope + vdim)
    w_ukv_p = jnp.concatenate([w_ukv_h[:, :, :nope].reshape(kv_rank, heads * nope),
                               w_ukv_h[:, :, nope:].reshape(kv_rank, heads * vdim)], axis=1).astype(BF16)
    wq = _pad_cols(q_head_norm.reshape(1, dqk), hp)
    wk = _pad_cols(k_head_norm.reshape(1, dqk), hp)
    w_out_b = w_out.astype(BF16)

    def in_proj(x):
        xn = _rmsnorm(x, mix_norm, out_dtype=BF16)
        tm_mla = 512
        return (_matmul(xn, w_in_hg, out_dtype=F32, name="in_proj_hg"),
                _matmul(xn, w_in_mla, out_dtype=F32, tm_target=tm_mla, tn_target=mla_cols, name="in_proj_mla"))

    u_hg, u_mla = in_proj(h)
    m_hg, m_mla = in_proj(h_meta)

    o_hg = _hgrn2(u_hg, m_hg, lb, hg_out_norm.reshape(-1), rows=t, width=width, heads=heads_hg)

    kr_block = (q_rank + kv_rank) // V7X_LANES
    kv_block = q_rank // kv_rank

    def keys_values(u, pos0):
        n = u.shape[0]
        ckv = _rmsnorm(u, mla_kv_norm, out_dtype=BF16, col_block=kv_block, ncols=kv_rank)
        kv = _matmul(ckv, w_ukv_p, out_dtype=F32, name="kv_up")
        cos_t, sin_t = _rope_tables(pos0, n, rope_dim)
        return _kprep(kv, u, kr_block, cos_t, sin_t, wk, heads=heads, nope=nope, vdim=vdim, dqk=dqk,
                      rope_dim=rope_dim)

    k_real, v_real = keys_values(u_mla, n_meta)
    k_meta, v_meta = keys_values(m_mla, 0)
    meta_pad = ((0, 0), (0, V7X_LANES - n_meta), (0, 0))
    k_meta, v_meta = jnp.pad(k_meta, meta_pad), jnp.pad(v_meta, meta_pad)

    cq = _rmsnorm(u_mla, mla_q_norm, out_dtype=BF16, col_block=0, ncols=q_rank)
    qraw = _matmul(cq, w_uq_p, out_dtype=F32, name="q_up")
    cos_t, sin_t = _rope_tables(n_meta, t, rope_dim)
    q = _qprep(qraw, cos_t, sin_t, wq, heads=heads, nope=nope, dqk=dqk, rope_dim=rope_dim)
    o_m = _attention(q, k_real, v_real, k_meta, v_meta, n_meta=n_meta)

    return _matmul2_res(o_hg, o_m, w_out_b[:width], w_out_b[width:], h, rows=t)


def kernel(x, meta_tokens, lb_param, ffn1_norm, ffn1_w_gate, ffn1_w_up, ffn1_w_down, mix_norm, w_in, hg_out_norm, mla_q_norm, mla_w_uq, mla_kv_norm, mla_w_ukv, q_head_norm, k_head_norm, w_out, ffn2_norm, ffn2_w_gate, ffn2_w_up, ffn2_w_down, out_norm):
    bsz, t, d = x.shape
    assert bsz == 1
    depth = ffn1_norm.shape[0]
    assert depth == 1
    lb_all = jnp.cumsum(jax.nn.softmax(lb_param.astype(F32), axis=0), axis=0)

    h = x[0]
    h_meta = meta_tokens.astype(x.dtype)
    for l in range(depth):
        wg1, wu1, wd1 = (w[l].astype(BF16) for w in (ffn1_w_gate, ffn1_w_up, ffn1_w_down))
        wg2, wu2, wd2 = (w[l].astype(BF16) for w in (ffn2_w_gate, ffn2_w_up, ffn2_w_down))
        h = _swiglu_half_step(h, ffn1_norm[l], wg1, wu1, wd1)
        h_meta = _swiglu_half_step(h_meta, ffn1_norm[l], wg1, wu1, wd1)
        h = _mixer(h, h_meta, lb_all[l], mix_norm[l], w_in[l], hg_out_norm[l], mla_q_norm[l], mla_w_uq[l],
                   mla_kv_norm[l], mla_w_ukv[l], q_head_norm[l], k_head_norm[l], w_out[l])
        h = _swiglu_half_step(h, ffn2_norm[l], wg2, wu2, wd2)
        h = _rmsnorm(h, out_norm[l], out_dtype=x.dtype)
    return h[None]
```
